```python
import jax, jax.numpy as jnp
from jax import lax
import numpy as np

D_MODEL = 1024
BATCH = 4
SEQ = 8192
DEPTH = 1

RET_HEADS = D_MODEL // 256
RET_QK_DIM = D_MODEL
RET_V_DIM = 2 * D_MODEL
RET_HEAD_QK = RET_QK_DIM // RET_HEADS
RET_HEAD_V = RET_V_DIM // RET_HEADS
RET_CHUNK = 128
ROPE_BASE = 10000.0
GN_EPS = 1e-5
CONV_DIM = D_MODEL
CONV_WIDTH = 31
LN_EPS = 1e-5
RMS_EPS = 1e-6
IN_SPLITS = (RET_QK_DIM, RET_QK_DIM, RET_V_DIM, RET_V_DIM, CONV_DIM, CONV_DIM, CONV_DIM, D_MODEL, D_MODEL)
IN_WIDTH = sum(IN_SPLITS)

kernel_name = "hybrid_retention_conformer_gated_block"


def rmsnorm(x, g):
    xf = x.astype(jnp.float32)
    y = xf * lax.rsqrt(jnp.mean(xf * xf, axis=-1, keepdims=True) + RMS_EPS)
    return (y * g.astype(jnp.float32)).astype(x.dtype)


def layernorm(x, g, b):
    xf = x.astype(jnp.float32)
    mu = jnp.mean(xf, axis=-1, keepdims=True)
    var = jnp.mean(jnp.square(xf - mu), axis=-1, keepdims=True)
    y = (xf - mu) * lax.rsqrt(var + LN_EPS)
    return (y * g.astype(jnp.float32) + b.astype(jnp.float32)).astype(x.dtype)


def head_groupnorm(o):
    of = o.astype(jnp.float32)
    mu = jnp.mean(of, axis=-1, keepdims=True)
    var = jnp.mean(jnp.square(of - mu), axis=-1, keepdims=True)
    return (of - mu) * lax.rsqrt(var + GN_EPS)


def rotary(t, positions):
    dh = t.shape[-1]
    half = dh // 2
    inv_freq = ROPE_BASE ** (-jnp.arange(half, dtype=jnp.float32) / half)
    ang = positions.astype(jnp.float32)[..., None] * inv_freq
    cos = jnp.cos(ang)[:, :, None, :]
    sin = jnp.sin(ang)[:, :, None, :]
    t1, t2 = t[..., :half], t[..., half:]
    return jnp.concatenate([t1 * cos - t2 * sin, t1 * sin + t2 * cos], axis=-1)


def retention_chunkwise(q, k, v):
    B, S, H, dk = q.shape
    dv = v.shape[-1]
    C = RET_CHUNK
    N = S // C
    log_g = jnp.log1p(-jnp.exp2(-5.0 - jnp.arange(H, dtype=jnp.float32)))
    idx = jnp.arange(C, dtype=jnp.float32)
    diff = idx[:, None] - idx[None, :]
    causal = diff >= 0
    decay_mask = jnp.where(causal, jnp.exp(log_g[:, None, None] * jnp.where(causal, diff, 0.0)), 0.0)
    xi = jnp.exp(log_g[:, None] * (idx + 1.0))
    zeta = jnp.exp(log_g[:, None] * (C - 1.0 - idx))
    g_chunk = jnp.exp(log_g * C)

    def to_chunks(t):
        return t.reshape(B, N, C, H, t.shape[-1]).transpose(1, 0, 3, 2, 4)

    def step(state, inp):
        qc, kc, vc = inp
        scores = jnp.einsum('bhid,bhjd->bhij', qc, kc) * decay_mask[None]
        inner = jnp.einsum('bhij,bhje->bhie', scores, vc)
        cross = jnp.einsum('bhid,bhde->bhie', qc, state) * xi[None, :, :, None]
        new_state = state * g_chunk[None, :, None, None] + jnp.einsum(
            'bhjd,bhje->bhde', kc * zeta[None, :, :, None], vc)
        return new_state, inner + cross

    state0 = jnp.zeros((B, H, dk, dv), jnp.float32)
    _, o = lax.scan(step, state0, (to_chunks(q), to_chunks(k), to_chunks(v)))
    return o.transpose(1, 0, 3, 2, 4).reshape(B, S, H, dv)


def causal_depthwise_conv(u, w, b):
    out = lax.conv_general_dilated(
        u, w[:, None, :].astype(u.dtype), window_strides=(1,),
        padding=[(CONV_WIDTH - 1, 0)],
        dimension_numbers=('NWC', 'WIO', 'NWC'),
        feature_group_count=u.shape[-1])
    return out + b.astype(u.dtype)


def setup_inputs(seed: int = 0) -> dict:
    key = jax.random.key(seed)
    ks = jax.random.split(key, 16)
    f32 = jnp.float32
    x = jax.random.normal(ks[0], (BATCH, SEQ, D_MODEL), f32)
    c = jax.random.normal(ks[1], (BATCH, D_MODEL), f32)
    offsets = jax.random.randint(ks[2], (BATCH, 1), 0, 4096, dtype=jnp.int32)
    positions = offsets + jnp.arange(SEQ, dtype=jnp.int32)[None, :]
    w_ada = jax.random.normal(ks[3], (DEPTH, D_MODEL, 3 * D_MODEL), f32) * (0.1 * D_MODEL ** -0.5)
    b_ada = jax.random.normal(ks[4], (DEPTH, 3 * D_MODEL), f32) * 0.02
    pre_norm_g = 1.0 + 0.05 * jax.random.normal(ks[5], (DEPTH, D_MODEL), f32)
    w_in = jax.random.normal(ks[6], (DEPTH, D_MODEL, IN_WIDTH), f32) * D_MODEL ** -0.5
    conv_w = jax.random.normal(ks[7], (DEPTH, CONV_WIDTH, CONV_DIM), f32) * CONV_WIDTH ** -0.5
    conv_b = jax.random.normal(ks[8], (DEPTH, CONV_DIM), f32) * 0.02
    conv_ln_g = 1.0 + 0.05 * jax.random.normal(ks[9], (DEPTH, CONV_DIM), f32)
    conv_ln_b = jax.random.normal(ks[10], (DEPTH, CONV_DIM), f32) * 0.02
    w_ret_out = jax.random.normal(ks[11], (DEPTH, RET_V_DIM, D_MODEL), f32) * RET_V_DIM ** -0.5
    w_conv_out = jax.random.normal(ks[12], (DEPTH, CONV_DIM, D_MODEL), f32) * CONV_DIM ** -0.5
    w_out = jax.random.normal(ks[13], (DEPTH, D_MODEL, D_MODEL), f32) * D_MODEL ** -0.5
    post_norm_g = 1.0 + 0.05 * jax.random.normal(ks[14], (DEPTH, D_MODEL), f32)
    return {"x": x, "c": c, "positions": positions, "w_ada": w_ada, "b_ada": b_ada,
            "pre_norm_g": pre_norm_g, "w_in": w_in, "conv_w": conv_w, "conv_b": conv_b,
            "conv_ln_g": conv_ln_g, "conv_ln_b": conv_ln_b, "w_ret_out": w_ret_out,
            "w_conv_out": w_conv_out, "w_out": w_out, "post_norm_g": post_norm_g}


def reference(x, c, positions, w_ada, b_ada, pre_norm_g, w_in, conv_w, conv_b,
              conv_ln_g, conv_ln_b, w_ret_out, w_conv_out, w_out, post_norm_g):
    B, S, _ = x.shape
    split_at = [int(v) for v in np.cumsum(IN_SPLITS)[:-1]]
    for l in range(DEPTH):
        mod = c @ w_ada[l] + b_ada[l]
        shift, scale, gate = jnp.split(mod, 3, axis=-1)
        h = rmsnorm(x, pre_norm_g[l]) * (1.0 + scale[:, None, :]) + shift[:, None, :]

        proj = h @ w_in[l]
        q, k, v, z_ret, u_val, u_gate, z_conv, g_a, g_b = jnp.split(proj, split_at, axis=-1)

        qh = rotary(q.reshape(B, S, RET_HEADS, RET_HEAD_QK).astype(jnp.float32), positions)
        kh = rotary(k.reshape(B, S, RET_HEADS, RET_HEAD_QK).astype(jnp.float32), positions) * RET_HEAD_QK ** -0.5
        vh = v.reshape(B, S, RET_HEADS, RET_HEAD_V).astype(jnp.float32)
        ret = head_groupnorm(retention_chunkwise(qh, kh, vh)).reshape(B, S, RET_V_DIM).astype(x.dtype)
        y_a = (ret * jax.nn.silu(z_ret)) @ w_ret_out[l]

        a = u_val * jax.nn.sigmoid(u_gate)
        a = causal_depthwise_conv(a, conv_w[l], conv_b[l])
        a = jax.nn.silu(layernorm(a, conv_ln_g[l], conv_ln_b[l]))
        y_b = (a * jax.nn.silu(z_conv)) @ w_conv_out[l]

        merged = jax.nn.sigmoid(g_a) * y_a + jax.nn.sigmoid(g_b) * y_b
        y = merged @ w_out[l]

        x = x + gate[:, None, :] * rmsnorm(y, post_norm_g[l])
    return x
```

```python
import functools

import jax
import jax.numpy as jnp
import numpy as np
from jax import lax
from jax.experimental import pallas as pl
from jax.experimental.pallas import tpu as pltpu

D_MODEL = 1024
RET_HEADS = 4
HEAD_QK = 256
HEAD_V = 512
HALF_QK = HEAD_QK // 2
QK_DIM = RET_HEADS * HEAD_QK
V_DIM = RET_HEADS * HEAD_V
CONV_WIDTH = 31
ROPE_BASE = 10000.0
GN_EPS = 1e-5
LN_EPS = 1e-5
RMS_EPS = 1e-6

COL_Q = 0
COL_K = COL_Q + QK_DIM
COL_V = COL_K + QK_DIM
COL_ZRET = COL_V + V_DIM
COL_UVAL = COL_ZRET + V_DIM
COL_UGATE = COL_UVAL + D_MODEL
COL_ZCONV = COL_UGATE + D_MODEL
COL_GA = COL_ZCONV + D_MODEL
COL_GB = COL_GA + D_MODEL
IN_WIDTH = COL_GB + D_MODEL

SEQ_TILE = 256
HALO_ROWS = 32
CONV_ROW_BLOCK = 64
LANES = 128
V7X_VMEM_LIMIT_BYTES = 60000 * 1024

F32 = jnp.float32
BF16 = jnp.bfloat16


def _sigmoid(z):
    return 1.0 / (1.0 + jnp.exp(-z))


def _silu(z):
    return z * _sigmoid(z)


def _mm(a, b):
    return jnp.dot(a, b, preferred_element_type=F32)


def _adaln_kernel(c_ref, w_ref, b_ref, o_ref):
    o_ref[...] = jnp.dot(c_ref[...], w_ref[...], preferred_element_type=F32,
                         precision=lax.Precision.HIGHEST) + b_ref[...]


def _adaln(c, w_ada, b_ada):
    batch, d = c.shape
    n = w_ada.shape[1]
    bn = 1024
    return pl.pallas_call(
        _adaln_kernel,
        grid=(n // bn,),
        in_specs=[pl.BlockSpec((batch, d), lambda j: (0, 0)),
                  pl.BlockSpec((d, bn), lambda j: (0, j)),
                  pl.BlockSpec((1, bn), lambda j: (0, j))],
        out_specs=pl.BlockSpec((batch, bn), lambda j: (0, j)),
        out_shape=jax.ShapeDtypeStruct((batch, n), F32),
        name="adaln_mod",
    )(c, w_ada, b_ada.reshape(1, n))


def _block_kernel(g_chunk,
                  x_ref, mod_ref, pos_ref, invf_ref, gpre_ref, win_ref, cw_ref, cb_ref,
                  lng_ref, lnb_ref, wret_ref, wconv_ref, wout_ref, gpost_ref,
                  mask_ref, xi_ref, zeta_ref,
                  o_ref,
                  state_ref, abuf_ref, conv_ref, hb_ref, q_ref, k_ref, kz_ref, v_ref, gated_ref):
    T = SEQ_TILE
    D = D_MODEL

    @pl.when(pl.program_id(1) == 0)
    def _():
        state_ref[...] = jnp.zeros_like(state_ref)
        abuf_ref[0:HALO_ROWS, :] = jnp.zeros((HALO_ROWS, D), F32)

    x = x_ref[...]
    shift = mod_ref[:, 0:D]
    scale = mod_ref[:, D:2 * D]
    ms = jnp.mean(x * x, axis=-1, keepdims=True)
    h = (x * lax.rsqrt(ms + RMS_EPS) * gpre_ref[...]) * (1.0 + scale) + shift
    hb_ref[...] = h.astype(BF16)

    ang = pos_ref[...].astype(F32) * invf_ref[...]
    cos = jnp.cos(ang)
    sin = jnp.sin(ang)

    k_scale = HEAD_QK ** -0.5
    for which in range(2):
        proj = _mm(hb_ref[...], win_ref[:, which * QK_DIM:(which + 1) * QK_DIM])
        for hh in range(RET_HEADS):
            t1 = proj[:, hh * HEAD_QK:hh * HEAD_QK + HALF_QK]
            t2 = proj[:, hh * HEAD_QK + HALF_QK:(hh + 1) * HEAD_QK]
            r1 = t1 * cos - t2 * sin
            r2 = t1 * sin + t2 * cos
            if which == 0:
                q_ref[hh, :, 0:HALF_QK] = r1.astype(BF16)
                q_ref[hh, :, HALF_QK:HEAD_QK] = r2.astype(BF16)
            else:
                r1 = r1 * k_scale
                r2 = r2 * k_scale
                k_ref[hh, :, 0:HALF_QK] = r1.astype(BF16)
                k_ref[hh, :, HALF_QK:HEAD_QK] = r2.astype(BF16)
                kz_ref[hh, :, 0:HALF_QK] = (r1 * zeta_ref[hh, :, 0:HALF_QK]).astype(BF16)
                kz_ref[hh, :, HALF_QK:HEAD_QK] = (r2 * zeta_ref[hh, :, HALF_QK:HEAD_QK]).astype(BF16)

    v_ref[...] = _mm(hb_ref[...], win_ref[:, COL_V:COL_V + V_DIM]).astype(BF16)

    for hh in range(RET_HEADS):
        q = q_ref[hh]
        v = v_ref[:, hh * HEAD_V:(hh + 1) * HEAD_V]
        scores = lax.dot_general(q, k_ref[hh], (((1,), (1,)), ((), ())), preferred_element_type=F32)
        inner = _mm((scores * mask_ref[hh]).astype(BF16), v)
        state = state_ref[hh]
        cross = _mm(q, state.astype(BF16)) * xi_ref[hh]
        o = inner + cross
        kz_t = kz_ref[hh].astype(F32).T.astype(BF16)
        state_ref[hh] = state * g_chunk[hh] + _mm(kz_t, v)
        mu = jnp.mean(o, axis=-1, keepdims=True)
        oc = o - mu
        var = jnp.mean(oc * oc, axis=-1, keepdims=True)
        ret = oc * lax.rsqrt(var + GN_EPS)
        z = _mm(hb_ref[...], win_ref[:, COL_ZRET + hh * HEAD_V:COL_ZRET + (hh + 1) * HEAD_V])
        gated_ref[:, hh * HEAD_V:(hh + 1) * HEAD_V] = (ret * _silu(z)).astype(BF16)
    y_a = _mm(gated_ref[...], wret_ref[...])

    u_val = _mm(hb_ref[...], win_ref[:, COL_UVAL:COL_UVAL + D])
    u_gate = _mm(hb_ref[...], win_ref[:, COL_UGATE:COL_UGATE + D])
    abuf_ref[HALO_ROWS:HALO_ROWS + T, :] = u_val * _sigmoid(u_gate)

    first_tap_row = HALO_ROWS - (CONV_WIDTH - 1)
    for cblk in range(D // LANES):
        cols = slice(cblk * LANES, (cblk + 1) * LANES)
        taps = [jnp.broadcast_to(cw_ref[j:j + 1, cols], (CONV_ROW_BLOCK, LANES)) for j in range(CONV_WIDTH)]
        bias = jnp.broadcast_to(cb_ref[:, cols], (CONV_ROW_BLOCK, LANES))

        for r0 in range(0, T, CONV_ROW_BLOCK):
            acc = bias
            for j in range(CONV_WIDTH):
                acc = acc + taps[j] * abuf_ref[r0 + first_tap_row + j:r0 + first_tap_row + j + CONV_ROW_BLOCK, cols]
            conv_ref[r0:r0 + CONV_ROW_BLOCK, cols] = acc
    abuf_ref[0:HALO_ROWS, :] = abuf_ref[T:T + HALO_ROWS, :]

    cv = conv_ref[...]
    mu = jnp.mean(cv, axis=-1, keepdims=True)
    cc = cv - mu
    var = jnp.mean(cc * cc, axis=-1, keepdims=True)
    a_ln = _silu(cc * lax.rsqrt(var + LN_EPS) * lng_ref[...] + lnb_ref[...])
    z_conv = _mm(hb_ref[...], win_ref[:, COL_ZCONV:COL_ZCONV + D])
    y_b = _mm((a_ln * _silu(z_conv)).astype(BF16), wconv_ref[...])

    g_a = _mm(hb_ref[...], win_ref[:, COL_GA:COL_GA + D])
    g_b = _mm(hb_ref[...], win_ref[:, COL_GB:COL_GB + D])
    merged = _sigmoid(g_a) * y_a + _sigmoid(g_b) * y_b
    y = _mm(merged.astype(BF16), wout_ref[...])
    ms_y = jnp.mean(y * y, axis=-1, keepdims=True)
    gate = mod_ref[:, 2 * D:3 * D]
    o_ref[...] = x_ref[...] + gate * (y * lax.rsqrt(ms_y + RMS_EPS) * gpost_ref[...])


def _decay_tables(T):
    log_g = np.log1p(-np.exp2(-5.0 - np.arange(RET_HEADS, dtype=np.float64)))
    idx = np.arange(T, dtype=np.float64)
    diff = idx[:, None] - idx[None, :]
    mask = np.where(diff >= 0, np.exp(log_g[:, None, None] * np.maximum(diff, 0.0)), 0.0)
    xi = np.exp(log_g[:, None] * (idx + 1.0))
    zeta = np.exp(log_g[:, None] * (T - 1.0 - idx))
    g_chunk = np.exp(log_g * T)
    xi_b = np.broadcast_to(xi[:, :, None], (RET_HEADS, T, HEAD_V))
    zeta_b = np.broadcast_to(zeta[:, :, None], (RET_HEADS, T, HEAD_QK))
    return (jnp.asarray(mask, F32), jnp.asarray(xi_b, F32), jnp.asarray(zeta_b, F32),
            tuple(float(g) for g in g_chunk))


def _resident(shape):
    zeros = (0,) * len(shape)
    return pl.BlockSpec(shape, lambda b, t: zeros, pipeline_mode=pl.Buffered(1))


def _block_layer(x, mod, positions, pre_g, w_in, conv_w, conv_b, ln_g, ln_b, w_ret_out, w_conv_out, w_out, post_g):
    B, S, D = x.shape
    T = SEQ_TILE
    mask, xi_b, zeta_b, g_chunk = _decay_tables(T)
    inv_freq = (ROPE_BASE ** (-jnp.arange(HALF_QK, dtype=F32) / HALF_QK)).reshape(1, HALF_QK)
    row = lambda a: a.reshape(1, -1)

    in_specs = [
        pl.BlockSpec((None, T, D), lambda b, t: (b, t, 0)),
        pl.BlockSpec((None, 1, 3 * D), lambda b, t: (b, 0, 0)),
        pl.BlockSpec((None, T, 1), lambda b, t: (b, t, 0)),
        _resident((1, HALF_QK)),
        _resident((1, D)),
        _resident((D, IN_WIDTH)),
        _resident((CONV_WIDTH, D)),
        _resident((1, D)),
        _resident((1, D)),
        _resident((1, D)),
        _resident((V_DIM, D)),
        _resident((D, D)),
        _resident((D, D)),
        _resident((1, D)),
        _resident((RET_HEADS, T, T)),
        _resident((RET_HEADS, T, HEAD_V)),
        _resident((RET_HEADS, T, HEAD_QK)),
    ]
    scratch = [
        pltpu.VMEM((RET_HEADS, HEAD_QK, HEAD_V), F32),
        pltpu.VMEM((HALO_ROWS + T, D), F32),
        pltpu.VMEM((T, D), F32),
        pltpu.VMEM((T, D), BF16),
        pltpu.VMEM((RET_HEADS, T, HEAD_QK), BF16),
        pltpu.VMEM((RET_HEADS, T, HEAD_QK), BF16),
        pltpu.VMEM((RET_HEADS, T, HEAD_QK), BF16),
        pltpu.VMEM((T, V_DIM), BF16),
        pltpu.VMEM((T, V_DIM), BF16),
    ]
    return pl.pallas_call(
        functools.partial(_block_kernel, g_chunk),
        grid=(B, S // T),
        in_specs=in_specs,
        out_specs=pl.BlockSpec((None, T, D), lambda b, t: (b, t, 0)),
        out_shape=jax.ShapeDtypeStruct((B, S, D), x.dtype),
        scratch_shapes=scratch,
        compiler_params=pltpu.CompilerParams(
            dimension_semantics=("arbitrary", "arbitrary"),
            vmem_limit_bytes=V7X_VMEM_LIMIT_BYTES),
        name="fused_block",
    )(x, mod.reshape(B, 1, 3 * D), positions.reshape(B, S, 1), inv_freq, row(pre_g),
      w_in.astype(BF16), conv_w, row(conv_b), row(ln_g), row(ln_b),
      w_ret_out.astype(BF16), w_conv_out.astype(BF16), w_out.astype(BF16), row(post_g),
      mask, xi_b, zeta_b)


def kernel(x, c, positions, w_ada, b_ada, pre_norm_g, w_in, conv_w, conv_b, conv_ln_g, conv_ln_b,
           w_ret_out, w_conv_out, w_out, post_norm_g):
    depth = w_in.shape[0]
    assert x.shape[1] % SEQ_TILE == 0 and x.shape[2] == D_MODEL and w_in.shape[2] == IN_WIDTH
    for l in range(depth):
        mod = _adaln(c, w_ada[l], b_ada[l])
        x = _block_layer(x, mod, positions, pre_norm_g[l], w_in[l], conv_w[l], conv_b[l],
                         conv_ln_g[l], conv_ln_b[l], w_ret_out[l], w_conv_out[l], w_out[l], post_norm_g[l])
    return x
```

```python
import functools

import jax
import jax.numpy as jnp
import numpy as np
from jax import lax
from jax.experimental import pallas as pl
from jax.experimental.pallas import tpu as pltpu

D_MODEL = 1024
RET_HEADS = 4
HEAD_QK = 256
HEAD_V = 512
HALF_QK = HEAD_QK // 2
QK_DIM = RET_HEADS * HEAD_QK
V_DIM = RET_HEADS * HEAD_V
CONV_WIDTH = 31
ROPE_BASE = 10000.0
GN_EPS = 1e-5
LN_EPS = 1e-5
RMS_EPS = 1e-6

COL_Q = 0
COL_K = COL_Q + QK_DIM
COL_V = COL_K + QK_DIM
COL_ZRET = COL_V + V_DIM
COL_UVAL = COL_ZRET + V_DIM
COL_UGATE = COL_UVAL + D_MODEL
COL_ZCONV = COL_UGATE + D_MODEL
COL_GA = COL_ZCONV + D_MODEL
COL_GB = COL_GA + D_MODEL
IN_WIDTH = COL_GB + D_MODEL

SEQ_TILE = 256
HALO_ROWS = 32
LANES = 128
SUBLANES = 8
V7X_VMEM_LIMIT_BYTES = 60000 * 1024

F32 = jnp.float32
BF16 = jnp.bfloat16


def _sigmoid(z):
    return 1.0 / (1.0 + jnp.exp(-z))


def _silu(z):
    return z * _sigmoid(z)


def _mm(a, b):
    return jnp.dot(a, b, preferred_element_type=F32)


def _adaln_kernel(c_ref, w_ref, b_ref, o_ref):
    o_ref[...] = jnp.dot(c_ref[...], w_ref[...], preferred_element_type=F32,
                         precision=lax.Precision.HIGHEST) + b_ref[...]


def _adaln(c, w_ada, b_ada):
    batch, d = c.shape
    n = w_ada.shape[1]
    bn = 1024
    return pl.pallas_call(
        _adaln_kernel,
        grid=(n // bn,),
        in_specs=[pl.BlockSpec((batch, d), lambda j: (0, 0)),
                  pl.BlockSpec((d, bn), lambda j: (0, j)),
                  pl.BlockSpec((1, bn), lambda j: (0, j))],
        out_specs=pl.BlockSpec((batch, bn), lambda j: (0, j)),
        out_shape=jax.ShapeDtypeStruct((batch, n), F32),
        name="adaln_mod",
    )(c, w_ada, b_ada.reshape(1, n))


def _block_kernel(g_chunk,
                  x_ref, mod_ref, pos_ref, invf_ref, gpre_ref, win_ref, cw_ref, cb_ref,
                  lng_ref, lnb_ref, wret_ref, wconv_ref, wout_ref, gpost_ref,
                  mask_ref, xi_ref, zeta_ref,
                  o_ref,
                  state_ref, abuf_ref, conv_ref, hb_ref, q_ref, k_ref, kz_ref, v_ref, gated_ref):
    T = SEQ_TILE
    D = D_MODEL

    @pl.when(pl.program_id(1) == 0)
    def _():
        state_ref[...] = jnp.zeros_like(state_ref)
        abuf_ref[:, 0:HALO_ROWS, :] = jnp.zeros((D // LANES, HALO_ROWS, LANES), F32)

    x = x_ref[...]
    shift = mod_ref[:, 0:D]
    scale = mod_ref[:, D:2 * D]
    ms = jnp.mean(x * x, axis=-1, keepdims=True)
    h = (x * lax.rsqrt(ms + RMS_EPS) * gpre_ref[...]) * (1.0 + scale) + shift
    hb_ref[...] = h.astype(BF16)

    ang = pos_ref[...].astype(F32) * invf_ref[...]
    cos = jnp.cos(ang)
    sin = jnp.sin(ang)

    k_scale = HEAD_QK ** -0.5
    for which in range(2):
        proj = _mm(hb_ref[...], win_ref[:, which * QK_DIM:(which + 1) * QK_DIM])
        for hh in range(RET_HEADS):
            t1 = proj[:, hh * HEAD_QK:hh * HEAD_QK + HALF_QK]
            t2 = proj[:, hh * HEAD_QK + HALF_QK:(hh + 1) * HEAD_QK]
            r1 = t1 * cos - t2 * sin
            r2 = t1 * sin + t2 * cos
            if which == 0:
                q_ref[hh, :, 0:HALF_QK] = r1.astype(BF16)
                q_ref[hh, :, HALF_QK:HEAD_QK] = r2.astype(BF16)
            else:
                r1 = r1 * k_scale
                r2 = r2 * k_scale
                k_ref[hh, :, 0:HALF_QK] = r1.astype(BF16)
                k_ref[hh, :, HALF_QK:HEAD_QK] = r2.astype(BF16)
                kz_ref[hh, :, 0:HALF_QK] = (r1 * zeta_ref[hh, :, 0:HALF_QK]).astype(BF16)
                kz_ref[hh, :, HALF_QK:HEAD_QK] = (r2 * zeta_ref[hh, :, HALF_QK:HEAD_QK]).astype(BF16)

    v_ref[...] = _mm(hb_ref[...], win_ref[:, COL_V:COL_V + V_DIM]).astype(BF16)

    u_val = _mm(hb_ref[...], win_ref[:, COL_UVAL:COL_UVAL + D])
    u_gate = _mm(hb_ref[...], win_ref[:, COL_UGATE:COL_UGATE + D])
    glu = u_val * _sigmoid(u_gate)
    for cblk in range(D // LANES):
        abuf_ref[cblk, HALO_ROWS:HALO_ROWS + T, :] = glu[:, cblk * LANES:(cblk + 1) * LANES]

    def conv_slab(cblk):
        cols = slice(cblk * LANES, (cblk + 1) * LANES)
        tap = lambda j: jnp.broadcast_to(cw_ref[j:j + 1, cols], (SUBLANES, LANES))
        bias = jnp.broadcast_to(cb_ref[:, cols], (SUBLANES, LANES))
        first_tap_row = HALO_ROWS - (CONV_WIDTH - 1)
        group_rows = 2 * SUBLANES
        accs = {}
        for s in range(T - group_rows + 1 + CONV_WIDTH):
            window = abuf_ref[cblk, pl.ds(first_tap_row + s, SUBLANES, stride=2), :]
            for t0 in range(0, T, group_rows):
                for phase in range(2):
                    j = s - t0 - phase
                    if 0 <= j < CONV_WIDTH:
                        prev = bias if j == 0 else accs.pop((t0, phase))
                        acc = prev + tap(j) * window
                        if j == CONV_WIDTH - 1:
                            conv_ref[cblk, pl.ds(t0 + phase, SUBLANES, stride=2), :] = acc
                        else:
                            accs[(t0, phase)] = acc
        assert not accs
        abuf_ref[cblk, 0:HALO_ROWS, :] = abuf_ref[cblk, T:T + HALO_ROWS, :]

    slabs_per_head = (D // LANES) // RET_HEADS
    for hh in range(RET_HEADS):
        q = q_ref[hh]
        v = v_ref[:, hh * HEAD_V:(hh + 1) * HEAD_V]
        scores = lax.dot_general(q, k_ref[hh], (((1,), (1,)), ((), ())), preferred_element_type=F32)
        state = state_ref[hh]
        cross = _mm(q, state.astype(BF16)) * xi_ref[hh]
        upd = lax.dot_general(kz_ref[hh], v, (((0,), (0,)), ((), ())), preferred_element_type=F32)
        state_ref[hh] = state * g_chunk[hh] + upd
        z = _mm(hb_ref[...], win_ref[:, COL_ZRET + hh * HEAD_V:COL_ZRET + (hh + 1) * HEAD_V])
        inner = _mm((scores * mask_ref[hh]).astype(BF16), v)
        o = inner + cross
        mu = jnp.mean(o, axis=-1, keepdims=True)
        oc = o - mu
        var = jnp.mean(oc * oc, axis=-1, keepdims=True)
        ret = oc * lax.rsqrt(var + GN_EPS)
        gated_ref[:, hh * HEAD_V:(hh + 1) * HEAD_V] = (ret * _silu(z)).astype(BF16)
        for cblk in range(hh * slabs_per_head, (hh + 1) * slabs_per_head):
            conv_slab(cblk)
    z_conv = _mm(hb_ref[...], win_ref[:, COL_ZCONV:COL_ZCONV + D])
    cv = jnp.concatenate([conv_ref[cblk] for cblk in range(D // LANES)], axis=-1)
    mu = jnp.mean(cv, axis=-1, keepdims=True)
    cc = cv - mu
    var = jnp.mean(cc * cc, axis=-1, keepdims=True)
    a_ln = _silu(cc * lax.rsqrt(var + LN_EPS) * lng_ref[...] + lnb_ref[...])
    gated_b = (a_ln * _silu(z_conv)).astype(BF16)
    y_a = _mm(gated_ref[...], wret_ref[...])
    g_a = _mm(hb_ref[...], win_ref[:, COL_GA:COL_GA + D])
    y_b = _mm(gated_b, wconv_ref[...])
    g_b = _mm(hb_ref[...], win_ref[:, COL_GB:COL_GB + D])

    merged = _sigmoid(g_a) * y_a + _sigmoid(g_b) * y_b
    y = _mm(merged.astype(BF16), wout_ref[...])
    ms_y = jnp.mean(y * y, axis=-1, keepdims=True)
    gate = mod_ref[:, 2 * D:3 * D]
    o_ref[...] = x_ref[...] + gate * (y * lax.rsqrt(ms_y + RMS_EPS) * gpost_ref[...])


def _decay_tables(T):
    log_g = np.log1p(-np.exp2(-5.0 - np.arange(RET_HEADS, dtype=np.float64)))
    idx = np.arange(T, dtype=np.float64)
    diff = idx[:, None] - idx[None, :]
    mask = np.where(diff >= 0, np.exp(log_g[:, None, None] * np.maximum(diff, 0.0)), 0.0)
    xi = np.exp(log_g[:, None] * (idx + 1.0))
    zeta = np.exp(log_g[:, None] * (T - 1.0 - idx))
    g_chunk = np.exp(log_g * T)
    xi_b = np.broadcast_to(xi[:, :, None], (RET_HEADS, T, HEAD_V))
    zeta_b = np.broadcast_to(zeta[:, :, None], (RET_HEADS, T, HEAD_QK))
    return (jnp.asarray(mask, F32), jnp.asarray(xi_b, F32), jnp.asarray(zeta_b, F32),
            tuple(float(g) for g in g_chunk))


def _resident(shape):
    zeros = (0,) * len(shape)
    return pl.BlockSpec(shape, lambda b, t: zeros, pipeline_mode=pl.Buffered(1))


def _block_layer(x, mod, positions, pre_g, w_in, conv_w, conv_b, ln_g, ln_b, w_ret_out, w_conv_out, w_out, post_g):
    B, S, D = x.shape
    T = SEQ_TILE
    mask, xi_b, zeta_b, g_chunk = _decay_tables(T)
    inv_freq = (ROPE_BASE ** (-jnp.arange(HALF_QK, dtype=F32) / HALF_QK)).reshape(1, HALF_QK)
    row = lambda a: a.reshape(1, -1)

    in_specs = [
        pl.BlockSpec((None, T, D), lambda b, t: (b, t, 0)),
        pl.BlockSpec((None, 1, 3 * D), lambda b, t: (b, 0, 0)),
        pl.BlockSpec((None, T, 1), lambda b, t: (b, t, 0)),
        _resident((1, HALF_QK)),
        _resident((1, D)),
        _resident((D, IN_WIDTH)),
        _resident((CONV_WIDTH, D)),
        _resident((1, D)),
        _resident((1, D)),
        _resident((1, D)),
        _resident((V_DIM, D)),
        _resident((D, D)),
        _resident((D, D)),
        _resident((1, D)),
        _resident((RET_HEADS, T, T)),
        _resident((RET_HEADS, T, HEAD_V)),
        _resident((RET_HEADS, T, HEAD_QK)),
    ]
    scratch = [
        pltpu.VMEM((RET_HEADS, HEAD_QK, HEAD_V), F32),
        pltpu.VMEM((D // LANES, HALO_ROWS + T, LANES), F32),
        pltpu.VMEM((D // LANES, T, LANES), F32),
        pltpu.VMEM((T, D), BF16),
        pltpu.VMEM((RET_HEADS, T, HEAD_QK), BF16),
        pltpu.VMEM((RET_HEADS, T, HEAD_QK), BF16),
        pltpu.VMEM((RET_HEADS, T, HEAD_QK), BF16),
        pltpu.VMEM((T, V_DIM), BF16),
        pltpu.VMEM((T, V_DIM), BF16),
    ]
    return pl.pallas_call(
        functools.partial(_block_kernel, g_chunk),
        grid=(B, S // T),
        in_specs=in_specs,
        out_specs=pl.BlockSpec((None, T, D), lambda b, t: (b, t, 0)),
        out_shape=jax.ShapeDtypeStruct((B, S, D), x.dtype),
        scratch_shapes=scratch,
        compiler_params=pltpu.CompilerParams(
            dimension_semantics=("arbitrary", "arbitrary"),
            vmem_limit_bytes=V7X_VMEM_LIMIT_BYTES),
        name="fused_block",
    )(x, mod.reshape(B, 1, 3 * D), positions.reshape(B, S, 1), inv_freq, row(pre_g),
      w_in.astype(BF16), conv_w, row(conv_b), row(ln_g), row(ln_b),
      w_ret_out.astype(BF16), w_conv_out.astype(BF16), w_out.astype(BF16), row(post_g),
      mask, xi_b, zeta_b)


def kernel(x, c, positions, w_ada, b_ada, pre_norm_g, w_in, conv_w, conv_b, conv_ln_g, conv_ln_b,
           w_ret_out, w_conv_out, w_out, post_norm_g):
    depth = w_in.shape[0]
    assert x.shape[1] % SEQ_TILE == 0 and x.shape[2] == D_MODEL and w_in.shape[2] == IN_WIDTH
    for l in range(depth):
        mod = _adaln(c, w_ada[l], b_ada[l])
        x = _block_layer(x, mod, positions, pre_norm_g[l], w_in[l], conv_w[l], conv_b[l],
                         conv_ln_g[l], conv_ln_b[l], w_ret_out[l], w_conv_out[l], w_out[l], post_norm_g[l])
    return x
```

```python
import functools

import jax
import jax.numpy as jnp
import numpy as np
from jax import lax
from jax.experimental import pallas as pl
from jax.experimental.pallas import tpu as pltpu

D_MODEL = 1024
RET_HEADS = 4
HEAD_QK = 256
HEAD_V = 512
HALF_QK = HEAD_QK // 2
QK_DIM = RET_HEADS * HEAD_QK
V_DIM = RET_HEADS * HEAD_V
CONV_WIDTH = 31
ROPE_BASE = 10000.0
GN_EPS = 1e-5
LN_EPS = 1e-5
RMS_EPS = 1e-6

COL_Q = 0
COL_K = COL_Q + QK_DIM
COL_V = COL_K + QK_DIM
COL_ZRET = COL_V + V_DIM
COL_UVAL = COL_ZRET + V_DIM
COL_UGATE = COL_UVAL + D_MODEL
COL_ZCONV = COL_UGATE + D_MODEL
COL_GA = COL_ZCONV + D_MODEL
COL_GB = COL_GA + D_MODEL
IN_WIDTH = COL_GB + D_MODEL

SEQ_TILE = 256
HALO_ROWS = 32
LANES = 128
SUBLANES = 8
N_SLABS = D_MODEL // LANES
MXU_COLS = 256
CONV_STEPS_PER_PIECE = 34
CONV_SLAB_SPLIT = (3, 2, 3)
V7X_VMEM_LIMIT_BYTES = 60000 * 1024

F32 = jnp.float32
BF16 = jnp.bfloat16
U32 = jnp.uint32


def _sigmoid(z):
    return 1.0 / (1.0 + jnp.exp(-z))


def _silu(z):
    return z * _sigmoid(z)


def _mm(a, b):
    return jnp.dot(a, b, preferred_element_type=F32)


def _unpack(words):
    return pltpu.bitcast(words, BF16)


def _pack(values_bf16):
    return pltpu.bitcast(values_bf16, U32)


def _pack_rows_bf16(w):
    k, n = w.shape
    halves = lax.bitcast_convert_type(w.astype(BF16), jnp.uint16).reshape(k // 2, 2, n)
    return lax.bitcast_convert_type(jnp.swapaxes(halves, -1, -2), U32)


def _weave(main, n_main, filler, n_filler):
    per_chunk = -(-n_filler // n_main)
    for _ in main:
        for _ in range(per_chunk):
            next(filler, None)
    for _ in filler:
        pass


def _adaln_kernel(c_ref, w_ref, b_ref, o_ref):
    o_ref[...] = jnp.dot(c_ref[...], w_ref[...], preferred_element_type=F32,
                         precision=lax.Precision.HIGHEST) + b_ref[...]


def _adaln(c, w_ada, b_ada):
    batch, d = c.shape
    n = w_ada.shape[1]
    bn = 1024
    return pl.pallas_call(
        _adaln_kernel,
        grid=(n // bn,),
        in_specs=[pl.BlockSpec((batch, d), lambda j: (0, 0)),
                  pl.BlockSpec((d, bn), lambda j: (0, j)),
                  pl.BlockSpec((1, bn), lambda j: (0, j))],
        out_specs=pl.BlockSpec((batch, bn), lambda j: (0, j)),
        out_shape=jax.ShapeDtypeStruct((batch, n), F32),
        name="adaln_mod",
    )(c, w_ada, b_ada.reshape(1, n))


def _block_kernel(g_chunk,
                  x_ref, mod_ref, pos_ref, invf_ref, gpre_ref, win_ref, cw_ref, cb_ref,
                  lng_ref, lnb_ref, wret_ref, wconv_ref, wout_ref, gpost_ref,
                  mask_ref, xi_ref, zeta_ref,
                  o_ref,
                  state_ref, abuf_ref, conv_ref, hb_ref, q_ref, k_ref, kz_ref, v_ref, gated_ref,
                  cos_ref, sin_ref, zc_ref, pa_ref, sgb_ref):
    T = SEQ_TILE
    D = D_MODEL
    in_region = pl.program_id(1) >= 0

    @pl.when(pl.program_id(1) == 0)
    def _():
        state_ref[...] = jnp.zeros_like(state_ref)
        abuf_ref[:, 0:HALO_ROWS, :] = jnp.zeros((N_SLABS, HALO_ROWS, LANES), F32)

    def conv_slab_pieces(cblk):
        first_tap_row = HALO_ROWS - (CONV_WIDTH - 1)
        group_rows = 2 * SUBLANES
        cols = slice(cblk * LANES, (cblk + 1) * LANES)
        bias = jnp.broadcast_to(cb_ref[:, cols], (SUBLANES, LANES))
        accs = {}
        for s in range(T - group_rows + 1 + CONV_WIDTH):
            window = abuf_ref[cblk, pl.ds(first_tap_row + s, SUBLANES, stride=2), :]
            for t0 in range(0, T, group_rows):
                for phase in range(2):
                    j = s - t0 - phase
                    if 0 <= j < CONV_WIDTH:
                        prev = bias if j == 0 else accs.pop((t0, phase))
                        acc = prev + jnp.broadcast_to(cw_ref[j:j + 1, cols], (SUBLANES, LANES)) * window
                        if j == CONV_WIDTH - 1:
                            conv_ref[cblk, pl.ds(t0 + phase, SUBLANES, stride=2), :] = acc
                        else:
                            accs[(t0, phase)] = acc
            if (s + 1) % CONV_STEPS_PER_PIECE == 0:
                yield
        assert not accs
        abuf_ref[cblk, 0:HALO_ROWS, :] = abuf_ref[cblk, T:T + HALO_ROWS, :]
        yield

    pieces_per_slab = (T + 2 * SUBLANES) // CONV_STEPS_PER_PIECE + 1

    def conv_pieces(slabs):
        for cblk in slabs:
            yield from conv_slab_pieces(cblk)

    slab_groups, first = [], 0
    for n in CONV_SLAB_SPLIT:
        slab_groups.append(range(first, first + n))
        first += n
    assert first == N_SLABS

    x = x_ref[...]
    shift = mod_ref[:, 0:D]
    scale = mod_ref[:, D:2 * D]
    ms = jnp.mean(x * x, axis=-1, keepdims=True)
    h = (x * lax.rsqrt(ms + RMS_EPS) * gpre_ref[...]) * (1.0 + scale) + shift
    hb_ref[...] = _pack(h.astype(BF16))

    ang = pos_ref[...].astype(F32) * invf_ref[...]
    cos_ref[...] = jnp.cos(ang)
    sin_ref[...] = jnp.sin(ang)

    u_val = _mm(_unpack(hb_ref[...]), _unpack(win_ref[:, COL_UVAL:COL_UVAL + D]))
    u_gate = _mm(_unpack(hb_ref[...]), _unpack(win_ref[:, COL_UGATE:COL_UGATE + D]))
    glu = u_val * _sigmoid(u_gate)
    for cblk in range(N_SLABS):
        abuf_ref[cblk, HALO_ROWS:HALO_ROWS + T, :] = glu[:, cblk * LANES:(cblk + 1) * LANES]

    @pl.when(in_region)
    def _():
        cos = cos_ref[...]
        sin = sin_ref[...]
        k_scale = HEAD_QK ** -0.5

        def projections():
            for hh in range(RET_HEADS):
                proj = _mm(_unpack(hb_ref[...]), _unpack(win_ref[:, COL_Q + hh * HEAD_QK:COL_Q + (hh + 1) * HEAD_QK]))
                t1 = proj[:, 0:HALF_QK]
                t2 = proj[:, HALF_QK:HEAD_QK]
                q_ref[hh, :, 0:HALF_QK] = _pack((t1 * cos - t2 * sin).astype(BF16))
                q_ref[hh, :, HALF_QK:HEAD_QK] = _pack((t1 * sin + t2 * cos).astype(BF16))
                yield
            for hh in range(RET_HEADS):
                proj = _mm(_unpack(hb_ref[...]), _unpack(win_ref[:, COL_K + hh * HEAD_QK:COL_K + (hh + 1) * HEAD_QK]))
                t1 = proj[:, 0:HALF_QK]
                t2 = proj[:, HALF_QK:HEAD_QK]
                r1 = (t1 * cos - t2 * sin) * k_scale
                r2 = (t1 * sin + t2 * cos) * k_scale
                k_ref[hh, :, 0:HALF_QK] = _pack(r1.astype(BF16))
                k_ref[hh, :, HALF_QK:HEAD_QK] = _pack(r2.astype(BF16))
                kz_ref[hh, :, 0:HALF_QK] = _pack((r1 * zeta_ref[hh, :, 0:HALF_QK]).astype(BF16))
                kz_ref[hh, :, HALF_QK:HEAD_QK] = _pack((r2 * zeta_ref[hh, :, HALF_QK:HEAD_QK]).astype(BF16))
                yield
            for blk in range(V_DIM // MXU_COLS):
                cols = slice(blk * MXU_COLS, (blk + 1) * MXU_COLS)
                v_ref[:, cols] = _pack(_mm(_unpack(hb_ref[...]), _unpack(win_ref[:, COL_V + blk * MXU_COLS:COL_V + (blk + 1) * MXU_COLS])).astype(BF16))
                yield

        _weave(projections(), 2 * RET_HEADS + V_DIM // MXU_COLS,
               conv_pieces(slab_groups[0]), len(slab_groups[0]) * pieces_per_slab)

    @pl.when(in_region)
    def _():
        def heads():
            for hh in range(RET_HEADS):
                q = _unpack(q_ref[hh])
                v = _unpack(v_ref[:, hh * HEAD_V:(hh + 1) * HEAD_V])
                scores = lax.dot_general(q, _unpack(k_ref[hh]), (((1,), (1,)), ((), ())), preferred_element_type=F32)
                yield
                state = state_ref[hh]
                cross = _mm(q, state.astype(BF16)) * xi_ref[hh]
                yield
                upd = lax.dot_general(_unpack(kz_ref[hh]), v, (((0,), (0,)), ((), ())), preferred_element_type=F32)
                state_ref[hh] = state * g_chunk[hh] + upd
                yield
                z = _mm(_unpack(hb_ref[...]), _unpack(win_ref[:, COL_ZRET + hh * HEAD_V:COL_ZRET + (hh + 1) * HEAD_V]))
                yield
                inner = _mm((scores * mask_ref[hh]).astype(BF16), v)
                o = inner + cross
                mu = jnp.mean(o, axis=-1, keepdims=True)
                oc = o - mu
                var = jnp.mean(oc * oc, axis=-1, keepdims=True)
                ret = oc * lax.rsqrt(var + GN_EPS)
                gated_ref[:, hh * HEAD_V:(hh + 1) * HEAD_V] = _pack((ret * _silu(z)).astype(BF16))
                yield

        _weave(heads(), 5 * RET_HEADS, conv_pieces(slab_groups[1]), len(slab_groups[1]) * pieces_per_slab)

    @pl.when(in_region)
    def _():
        def tail_projections():
            zc_ref[...] = _silu(_mm(_unpack(hb_ref[...]), _unpack(win_ref[:, COL_ZCONV:COL_ZCONV + D])))
            yield
            y_a = _mm(_unpack(gated_ref[...]), _unpack(wret_ref[...]))
            yield
            pa_ref[...] = _sigmoid(_mm(_unpack(hb_ref[...]), _unpack(win_ref[:, COL_GA:COL_GA + D]))) * y_a
            yield
            sgb_ref[...] = _sigmoid(_mm(_unpack(hb_ref[...]), _unpack(win_ref[:, COL_GB:COL_GB + D])))
            yield

        _weave(tail_projections(), 4, conv_pieces(slab_groups[2]), len(slab_groups[2]) * pieces_per_slab)

    gate = mod_ref[:, 2 * D:3 * D]
    for r0 in range(0, T, T // 2):
        rows = slice(r0, r0 + T // 2)
        cv = jnp.concatenate([conv_ref[cblk, rows, :] for cblk in range(N_SLABS)], axis=-1)
        mu = jnp.mean(cv, axis=-1, keepdims=True)
        cc = cv - mu
        var = jnp.mean(cc * cc, axis=-1, keepdims=True)
        a_ln = _silu(cc * lax.rsqrt(var + LN_EPS) * lng_ref[...] + lnb_ref[...])
        y_b = _mm((a_ln * zc_ref[rows, :]).astype(BF16), _unpack(wconv_ref[...]))
        merged = pa_ref[rows, :] + sgb_ref[rows, :] * y_b
        y = _mm(merged.astype(BF16), _unpack(wout_ref[...]))
        ms_y = jnp.mean(y * y, axis=-1, keepdims=True)
        o_ref[rows, :] = x_ref[rows, :] + gate * (y * lax.rsqrt(ms_y + RMS_EPS) * gpost_ref[...])


def _decay_tables(T):
    log_g = np.log1p(-np.exp2(-5.0 - np.arange(RET_HEADS, dtype=np.float64)))
    idx = np.arange(T, dtype=np.float64)
    diff = idx[:, None] - idx[None, :]
    mask = np.where(diff >= 0, np.exp(log_g[:, None, None] * np.maximum(diff, 0.0)), 0.0)
    xi = np.exp(log_g[:, None] * (idx + 1.0))
    zeta = np.exp(log_g[:, None] * (T - 1.0 - idx))
    g_chunk = np.exp(log_g * T)
    xi_b = np.broadcast_to(xi[:, :, None], (RET_HEADS, T, HEAD_V))
    zeta_b = np.broadcast_to(zeta[:, :, None], (RET_HEADS, T, HEAD_QK))
    return (jnp.asarray(mask, F32), jnp.asarray(xi_b, F32), jnp.asarray(zeta_b, F32),
            tuple(float(g) for g in g_chunk))


def _resident(shape):
    zeros = (0,) * len(shape)
    return pl.BlockSpec(shape, lambda b, t: zeros, pipeline_mode=pl.Buffered(1))


def _block_layer(x, mod, positions, pre_g, w_in, conv_w, conv_b, ln_g, ln_b, w_ret_out, w_conv_out, w_out, post_g):
    B, S, D = x.shape
    T = SEQ_TILE
    mask, xi_b, zeta_b, g_chunk = _decay_tables(T)
    inv_freq = (ROPE_BASE ** (-jnp.arange(HALF_QK, dtype=F32) / HALF_QK)).reshape(1, HALF_QK)
    row = lambda a: a.reshape(1, -1)

    in_specs = [
        pl.BlockSpec((None, T, D), lambda b, t: (b, t, 0)),
        pl.BlockSpec((None, 1, 3 * D), lambda b, t: (b, 0, 0)),
        pl.BlockSpec((None, T, 1), lambda b, t: (b, t, 0)),
        _resident((1, HALF_QK)),
        _resident((1, D)),
        _resident((D // 2, IN_WIDTH)),
        _resident((CONV_WIDTH, D)),
        _resident((1, D)),
        _resident((1, D)),
        _resident((1, D)),
        _resident((V_DIM // 2, D)),
        _resident((D // 2, D)),
        _resident((D // 2, D)),
        _resident((1, D)),
        _resident((RET_HEADS, T, T)),
        _resident((RET_HEADS, T, HEAD_V)),
        _resident((RET_HEADS, T, HEAD_QK)),
    ]
    scratch = [
        pltpu.VMEM((RET_HEADS, HEAD_QK, HEAD_V), F32),
        pltpu.VMEM((N_SLABS, HALO_ROWS + T, LANES), F32),
        pltpu.VMEM((N_SLABS, T, LANES), F32),
        pltpu.VMEM((T // 2, D), U32),
        pltpu.VMEM((RET_HEADS, T // 2, HEAD_QK), U32),
        pltpu.VMEM((RET_HEADS, T // 2, HEAD_QK), U32),
        pltpu.VMEM((RET_HEADS, T // 2, HEAD_QK), U32),
        pltpu.VMEM((T // 2, V_DIM), U32),
        pltpu.VMEM((T // 2, V_DIM), U32),
        pltpu.VMEM((T, HALF_QK), F32),
        pltpu.VMEM((T, HALF_QK), F32),
        pltpu.VMEM((T, D), F32),
        pltpu.VMEM((T, D), F32),
        pltpu.VMEM((T, D), F32),
    ]
    return pl.pallas_call(
        functools.partial(_block_kernel, g_chunk),
        grid=(B, S // T),
        in_specs=in_specs,
        out_specs=pl.BlockSpec((None, T, D), lambda b, t: (b, t, 0)),
        out_shape=jax.ShapeDtypeStruct((B, S, D), x.dtype),
        scratch_shapes=scratch,
        compiler_params=pltpu.CompilerParams(
            dimension_semantics=("arbitrary", "arbitrary"),
            vmem_limit_bytes=V7X_VMEM_LIMIT_BYTES),
        name="fused_block",
    )(x, mod.reshape(B, 1, 3 * D), positions.reshape(B, S, 1), inv_freq, row(pre_g),
      _pack_rows_bf16(w_in), conv_w, row(conv_b), row(ln_g), row(ln_b),
      _pack_rows_bf16(w_ret_out), _pack_rows_bf16(w_conv_out), _pack_rows_bf16(w_out), row(post_g),
      mask, xi_b, zeta_b)


def kernel(x, c, positions, w_ada, b_ada, pre_norm_g, w_in, conv_w, conv_b, conv_ln_g, conv_ln_b,
           w_ret_out, w_conv_out, w_out, post_norm_g):
    depth = w_in.shape[0]
    assert x.shape[1] % SEQ_TILE == 0 and x.shape[2] == D_MODEL and w_in.shape[2] == IN_WIDTH
    for l in range(depth):
        mod = _adaln(c, w_ada[l], b_ada[l])
        x = _block_layer(x, mod, positions, pre_norm_g[l], w_in[l], conv_w[l], conv_b[l],
                         conv_ln_g[l], conv_ln_b[l], w_ret_out[l], w_conv_out[l], w_out[l], post_norm_g[l])
    return x
```
